```python
import jax, jax.numpy as jnp
from jax import lax
import numpy as np

D_MODEL = 4096
BATCH = 2
SEQ = 8192
DEPTH = 2

N_BRANCH = 4
BRANCH_WIDTH = D_MODEL // 4
HEAD_DIM = 128
MOBA_HEADS = BRANCH_WIDTH // HEAD_DIM
MOBA_BLOCK = 256
MOBA_TOPK = 3
MOBA_Q_CHUNK = 32
FOX_HEADS = BRANCH_WIDTH // HEAD_DIM
FOX_Q_BLOCK = 128
POOL_WINDOWS = (2, 4, 8, 16)
POOL_GROUPS = len(POOL_WINDOWS)
POOL_GROUP_WIDTH = BRANCH_WIDTH // POOL_GROUPS
CONV_K = 3
RMS_EPS = 1e-6
NEG_INF = -1e30

_W = BRANCH_WIDTH
_IN_SEGMENTS = (3 * _W, _W,
                3 * _W, _W, FOX_HEADS,
                _W, _W,
                3 * _W, _W,
                N_BRANCH * D_MODEL)
IN_SPLITS = tuple(int(v) for v in np.cumsum(_IN_SEGMENTS)[:-1])
N_IN = int(sum(_IN_SEGMENTS))

kernel_name = "hybrid_moba_fox_pool_conv_gated_merge"


def rms_norm(x, g):
    x32 = x.astype(jnp.float32)
    y = x32 * lax.rsqrt(jnp.mean(x32 * x32, axis=-1, keepdims=True) + RMS_EPS)
    return (y * g.astype(jnp.float32)).astype(x.dtype)


def split_heads(t, n_heads):
    b, s, _ = t.shape
    return t.reshape(b, s, n_heads, HEAD_DIM).transpose(0, 2, 1, 3)


def merge_heads(t):
    b, h, s, d = t.shape
    return t.transpose(0, 2, 1, 3).reshape(b, s, h * d)


def moba_attention(q, k, v):
    b, h, s, d = q.shape
    nb = -(-s // MOBA_BLOCK)
    s_pad = nb * MOBA_BLOCK
    pad = ((0, 0), (0, 0), (0, s_pad - s), (0, 0))
    kb = jnp.pad(k, pad).reshape(b, h, nb, MOBA_BLOCK, d)
    vb = jnp.pad(v, pad).reshape(b, h, nb, MOBA_BLOCK, d)
    scale = HEAD_DIM ** -0.5
    k_mean = jnp.mean(kb.astype(jnp.float32), axis=3)
    gate = jnp.einsum("bhsd,bhnd->bhsn", q.astype(jnp.float32), k_mean)
    q_blk = jnp.arange(s) // MOBA_BLOCK
    past = jnp.arange(nb)[None, :] < q_blk[:, None]
    gate = jnp.where(past, gate, NEG_INF)
    topk = min(MOBA_TOPK, nb)
    _, idx = lax.top_k(gate, topk)
    valid = idx < q_blk[None, None, :, None]

    nc = s // MOBA_Q_CHUNK

    def to_chunks(t):
        t = t.reshape((b, h, nc, MOBA_Q_CHUNK) + t.shape[3:])
        return jnp.moveaxis(t, 2, 0)

    gather = jax.vmap(jax.vmap(lambda blocks, i: blocks[i]))

    def chunk(args):
        qc, ic, vc, c = args
        t_pos = c * MOBA_Q_CHUNK + jnp.arange(MOBA_Q_CHUNK)
        blk = (c * MOBA_Q_CHUNK) // MOBA_BLOCK
        k_sel = gather(kb, ic).reshape(b, h, MOBA_Q_CHUNK, topk * MOBA_BLOCK, d)
        v_sel = gather(vb, ic).reshape(b, h, MOBA_Q_CHUNK, topk * MOBA_BLOCK, d)
        s_sel = jnp.einsum("bhqd,bhqkd->bhqk", qc, k_sel).astype(jnp.float32) * scale
        s_sel = jnp.where(jnp.repeat(vc, MOBA_BLOCK, axis=-1), s_sel, NEG_INF)
        k_own = lax.dynamic_index_in_dim(kb, blk, axis=2, keepdims=False)
        v_own = lax.dynamic_index_in_dim(vb, blk, axis=2, keepdims=False)
        s_own = jnp.einsum("bhqd,bhkd->bhqk", qc, k_own).astype(jnp.float32) * scale
        own_pos = blk * MOBA_BLOCK + jnp.arange(MOBA_BLOCK)
        s_own = jnp.where(own_pos[None, :] <= t_pos[:, None], s_own, NEG_INF)
        p = jax.nn.softmax(jnp.concatenate([s_sel, s_own], axis=-1), axis=-1).astype(v.dtype)
        n_sel = topk * MOBA_BLOCK
        return (jnp.einsum("bhqk,bhqkd->bhqd", p[..., :n_sel], v_sel)
                + jnp.einsum("bhqk,bhkd->bhqd", p[..., n_sel:], v_own))

    out = lax.map(chunk, (to_chunks(q), to_chunks(idx), to_chunks(valid), jnp.arange(nc)))
    return jnp.moveaxis(out, 0, 2).reshape(b, h, s, d)


def fox_attention(q, k, v, log_f):
    b, h, s, d = q.shape
    scale = HEAD_DIM ** -0.5
    c = jnp.cumsum(log_f, axis=-1)
    nq = s // FOX_Q_BLOCK
    k_pos = jnp.arange(s)
    q_blocks = jnp.moveaxis(q.reshape(b, h, nq, FOX_Q_BLOCK, d), 2, 0)
    c_blocks = jnp.moveaxis(c.reshape(b, h, nq, FOX_Q_BLOCK), 2, 0)

    def block(args):
        qb, cb, i = args
        t_pos = i * FOX_Q_BLOCK + jnp.arange(FOX_Q_BLOCK)
        logits = (jnp.einsum("bhqd,bhkd->bhqk", qb, k).astype(jnp.float32) * scale
                  + cb[..., None] - c[:, :, None, :])
        logits = jnp.where(k_pos[None, :] <= t_pos[:, None], logits, NEG_INF)
        p = jax.nn.softmax(logits, axis=-1).astype(v.dtype)
        return jnp.einsum("bhqk,bhkd->bhqd", p, v)

    out = lax.map(block, (q_blocks, c_blocks, jnp.arange(nq)))
    return jnp.moveaxis(out, 0, 2).reshape(b, h, s, d)


def pool_mixer(u, w_pool, pool_scale):
    b, s, w_total = u.shape
    u32 = u.astype(jnp.float32)
    csum = jnp.concatenate([jnp.zeros((b, 1, w_total), jnp.float32), jnp.cumsum(u32, axis=1)], axis=1)
    t = jnp.arange(s)
    outs = []
    for g, win in enumerate(POOL_WINDOWS):
        sl = slice(g * POOL_GROUP_WIDTH, (g + 1) * POOL_GROUP_WIDTH)
        cg = csum[..., sl]
        lag = jnp.concatenate([jnp.zeros((b, win - 1, POOL_GROUP_WIDTH), jnp.float32),
                               cg[:, :s + 1 - win]], axis=1)
        cnt = jnp.minimum(t + 1, win).astype(jnp.float32)[None, :, None]
        outs.append((cg[:, 1:] - lag) / cnt - u32[..., sl])
    pooled = jnp.stack(outs, axis=2).astype(u.dtype)
    y = jnp.einsum("bsgc,gcd->bsgd", pooled, w_pool).reshape(b, s, w_total)
    return y * pool_scale


def short_conv_mixer(bcx, w_conv):
    b_g, c_g, xs = jnp.split(bcx, 3, axis=-1)
    z = c_g * xs
    width = z.shape[-1]
    y = lax.conv_general_dilated(z, w_conv[:, None, :].astype(z.dtype), window_strides=(1,),
                                 padding=[(CONV_K - 1, 0)],
                                 dimension_numbers=("NWC", "WIO", "NWC"),
                                 feature_group_count=width)
    return b_g * y


def hybrid_layer(x, g_norm, w_in, b_forget, w_pool, pool_scale, w_conv, w_branch, b_merge, w_out):
    b, s, d = x.shape
    hn = rms_norm(x, g_norm)
    proj = jnp.einsum("bsd,dn->bsn", hn, w_in)
    (a_qkv, a_gate, f_qkv, f_gate, f_forget, p_in, p_gate, c_bcx, c_gate,
     m_gate) = jnp.split(proj, IN_SPLITS, axis=-1)
    qa, ka, va = [split_heads(t, MOBA_HEADS) for t in jnp.split(a_qkv, 3, axis=-1)]
    o_a = merge_heads(moba_attention(qa, ka, va))
    qf, kf, vf = [split_heads(t, FOX_HEADS) for t in jnp.split(f_qkv, 3, axis=-1)]
    log_f = jax.nn.log_sigmoid((f_forget + b_forget).astype(jnp.float32)).transpose(0, 2, 1)
    o_b = merge_heads(fox_attention(qf, kf, vf, log_f))
    o_c = pool_mixer(p_in, w_pool, pool_scale)
    o_d = short_conv_mixer(c_bcx, w_conv)
    branches = (o_a * jax.nn.silu(a_gate), o_b * jax.nn.silu(f_gate),
                o_c * jax.nn.silu(p_gate), o_d * jax.nn.silu(c_gate))
    m_gate = m_gate.reshape(b, s, N_BRANCH, d) + b_merge
    merged = jnp.zeros_like(x)
    for i in range(N_BRANCH):
        up = jnp.einsum("bsw,wd->bsd", branches[i], w_branch[i])
        merged = merged + jax.nn.sigmoid(m_gate[:, :, i]) * up
    return x + jnp.einsum("bsd,de->bse", merged, w_out)


def setup_inputs(seed: int = 0) -> dict:
    key = jax.random.key(seed)
    ks = jax.random.split(key, 12)
    f32 = jnp.float32
    x = jax.random.normal(ks[0], (BATCH, SEQ, D_MODEL), f32)
    norm_g = 1.0 + 0.02 * jax.random.normal(ks[1], (DEPTH, D_MODEL), f32)
    w_in = jax.random.normal(ks[2], (DEPTH, D_MODEL, N_IN), f32) * D_MODEL ** -0.5
    b_forget = jax.random.uniform(ks[3], (DEPTH, FOX_HEADS), f32, minval=1.0, maxval=6.0)
    w_pool = jax.random.normal(ks[4], (DEPTH, POOL_GROUPS, POOL_GROUP_WIDTH, POOL_GROUP_WIDTH), f32) * POOL_GROUP_WIDTH ** -0.5
    pool_scale = 1.0 + 0.02 * jax.random.normal(ks[5], (DEPTH, BRANCH_WIDTH), f32)
    w_conv = jax.random.normal(ks[6], (DEPTH, CONV_K, BRANCH_WIDTH), f32) * CONV_K ** -0.5
    w_branch = jax.random.normal(ks[7], (DEPTH, N_BRANCH, BRANCH_WIDTH, D_MODEL), f32) * BRANCH_WIDTH ** -0.5
    b_merge = 0.02 * jax.random.normal(ks[8], (DEPTH, N_BRANCH, D_MODEL), f32)
    w_out = jax.random.normal(ks[9], (DEPTH, D_MODEL, D_MODEL), f32) * D_MODEL ** -0.5
    final_g = 1.0 + 0.02 * jax.random.normal(ks[10], (D_MODEL,), f32)
    return {"x": x, "norm_g": norm_g, "w_in": w_in, "b_forget": b_forget, "w_pool": w_pool,
            "pool_scale": pool_scale, "w_conv": w_conv, "w_branch": w_branch,
            "b_merge": b_merge, "w_out": w_out, "final_g": final_g}


def reference(x, norm_g, w_in, b_forget, w_pool, pool_scale, w_conv, w_branch, b_merge, w_out, final_g):
    h = x
    for layer in range(DEPTH):
        h = hybrid_layer(h, norm_g[layer], w_in[layer], b_forget[layer], w_pool[layer],
                         pool_scale[layer], w_conv[layer], w_branch[layer], b_merge[layer],
                         w_out[layer])
    return rms_norm(h, final_g)
```

```python
import functools

import jax
import jax.numpy as jnp
from jax import lax
from jax.experimental import pallas as pl
from jax.experimental.pallas import tpu as pltpu

D_MODEL = 4096
BATCH = 2
SEQ = 8192
DEPTH = 2
TOKENS = BATCH * SEQ
N_BRANCH = 4
BRANCH_WIDTH = D_MODEL // 4
HEAD_DIM = 128
HEADS = BRANCH_WIDTH // HEAD_DIM
MOBA_BLOCK = 256
MOBA_TOPK = 3
MOBA_NBLK = SEQ // MOBA_BLOCK
FOX_TILE = 512
POOL_WINDOWS = (2, 4, 8, 16)
POOL_GROUP_WIDTH = BRANCH_WIDTH // len(POOL_WINDOWS)
CONV_K = 3
RMS_EPS = 1e-6
NEG_INF = -1e30
BELOW_NEG_INF = -3e38
HALO = 16

SEG_A_Q, SEG_A_K, SEG_A_V, SEG_A_GATE = 0, 1, 2, 3
SEG_F_Q, SEG_F_K, SEG_F_V, SEG_F_GATE = 4, 5, 6, 7
SEG_P_IN, SEG_P_GATE, SEG_C_B, SEG_C_C, SEG_C_X, SEG_C_GATE = 8, 9, 10, 11, 12, 13
N_SEG = 14
FORGET_OFF = 2 * 4 * BRANCH_WIDTH
MERGE_OFF = FORGET_OFF + HEADS + 6 * BRANCH_WIDTH
LANES = 128

V7X_VMEM_LIMIT = 56 * 1024 * 1024

BF16 = jnp.bfloat16
F32 = jnp.float32


def _params(n_grid):
    return pltpu.CompilerParams(dimension_semantics=("arbitrary",) * n_grid,
                                vmem_limit_bytes=V7X_VMEM_LIMIT)


def _sigmoid(x):
    return 1.0 / (1.0 + jnp.exp(-x))


def _silu(x):
    return x * _sigmoid(x)


def _rmsnorm_kernel(x_ref, g_ref, o_ref):
    x = x_ref[...]
    ms = jnp.mean(x * x, axis=-1, keepdims=True)
    o_ref[...] = ((x * lax.rsqrt(ms + RMS_EPS)) * g_ref[...]).astype(o_ref.dtype)


def rmsnorm(x, g, tm=512):
    return pl.pallas_call(
        _rmsnorm_kernel,
        grid=(TOKENS // tm,),
        in_specs=[pl.BlockSpec((tm, D_MODEL), lambda i: (i, 0)),
                  pl.BlockSpec((1, D_MODEL), lambda i: (0, 0))],
        out_specs=pl.BlockSpec((tm, D_MODEL), lambda i: (i, 0)),
        out_shape=jax.ShapeDtypeStruct((TOKENS, D_MODEL), BF16),
        compiler_params=_params(1),
        name="rmsnorm",
    )(x, g.reshape(1, D_MODEL))


def _matmul_kernel(x_ref, w_ref, o_ref):
    o_ref[...] = jnp.dot(x_ref[...], w_ref[...],
                         preferred_element_type=F32).astype(o_ref.dtype)


def branch_projection(hn, w, tm=1024, tn=1024):
    n = w.shape[1]
    return pl.pallas_call(
        _matmul_kernel,
        grid=(TOKENS // tm, n // tn),
        in_specs=[pl.BlockSpec((tm, D_MODEL), lambda i, j: (i, 0)),
                  pl.BlockSpec((D_MODEL, tn), lambda i, j: (0, j))],
        out_specs=pl.BlockSpec((tm, tn), lambda i, j: (i, j)),
        out_shape=jax.ShapeDtypeStruct((TOKENS, n), BF16),
        compiler_params=_params(2),
        name="branch_projection",
    )(hn, w)


def _forget_kernel(hn_ref, w_ref, b_ref, c_ref, carry_ref, *, tm):
    i = pl.program_id(0)

    @pl.when((i * tm) % SEQ == 0)
    def _():
        carry_ref[...] = jnp.zeros_like(carry_ref)

    z = jnp.dot(hn_ref[...], w_ref[...], preferred_element_type=F32) + b_ref[...]
    logf = jnp.minimum(z, 0.0) - jnp.log(1.0 + jnp.exp(-jnp.abs(z)))
    r = lax.broadcasted_iota(jnp.int32, (tm, tm), 0)
    c = lax.broadcasted_iota(jnp.int32, (tm, tm), 1)
    tri = (c <= r).astype(BF16)
    hi = logf.astype(BF16)
    rem = logf - hi.astype(F32)
    mid = rem.astype(BF16)
    lo = (rem - mid.astype(F32)).astype(BF16)
    cum = (jnp.dot(tri, hi, preferred_element_type=F32)
           + jnp.dot(tri, mid, preferred_element_type=F32)
           + jnp.dot(tri, lo, preferred_element_type=F32))
    out = cum + carry_ref[...]
    c_ref[...] = out
    carry_ref[...] = out[tm - 1:tm, :]


def forget_cumsum(hn, wf, bf, tm=1024):
    return pl.pallas_call(
        functools.partial(_forget_kernel, tm=tm),
        grid=(TOKENS // tm,),
        in_specs=[pl.BlockSpec((tm, D_MODEL), lambda i: (i, 0)),
                  pl.BlockSpec((D_MODEL, LANES), lambda i: (0, 0)),
                  pl.BlockSpec((1, LANES), lambda i: (0, 0))],
        out_specs=pl.BlockSpec((tm, LANES), lambda i: (i, 0)),
        out_shape=jax.ShapeDtypeStruct((TOKENS, LANES), F32),
        scratch_shapes=[pltpu.VMEM((1, LANES), F32)],
        compiler_params=_params(1),
        name="forget_cumsum",
    )(hn, wf, bf)


def _qk(q, k):
    return lax.dot_general(q, k, (((1,), (1,)), ((), ())), preferred_element_type=F32)


def _softmax_first(s, v):
    m = jnp.max(s, axis=1, keepdims=True)
    p = jnp.exp(s - m)
    l = jnp.sum(p, axis=1, keepdims=True)
    acc = jnp.dot(p.astype(BF16), v, preferred_element_type=F32)
    return m, l, acc


def _softmax_update(s, v, carry):
    m, l, acc = carry
    m_new = jnp.maximum(m, jnp.max(s, axis=1, keepdims=True))
    alpha = jnp.exp(m - m_new)
    p = jnp.exp(s - m_new)
    l = alpha * l + jnp.sum(p, axis=1, keepdims=True)
    acc = alpha * acc + jnp.dot(p.astype(BF16), v, preferred_element_type=F32)
    return m_new, l, acc


def _fox_kernel(q_ref, k_ref, v_ref, g_ref, c_ref, o_ref, *, tile):
    i = pl.program_id(1)
    q = q_ref[...]
    scale = HEAD_DIM ** -0.5

    def scores(j):
        k = k_ref[pl.ds(pl.multiple_of(j * tile, tile), tile), :]
        return _qk(q, k) * scale - c_ref[j]

    def values(j):
        return v_ref[pl.ds(pl.multiple_of(j * tile, tile), tile), :]

    row = lax.broadcasted_iota(jnp.int32, (tile, tile), 0)
    col = lax.broadcasted_iota(jnp.int32, (tile, tile), 1)
    s = jnp.where(col <= row, scores(i), NEG_INF)
    carry = _softmax_first(s, values(i))
    m, l, acc = lax.fori_loop(
        0, i, lambda j, cr: _softmax_update(scores(j), values(j), cr), carry)
    g = g_ref[...].astype(F32)
    o_ref[...] = ((acc / l) * _silu(g)).astype(o_ref.dtype)


def _moba_kernel(q_ref, k_ref, v_ref, g_ref, o_ref, kmean_ref):
    i = pl.program_id(1)
    blk = MOBA_BLOCK

    @pl.when(i == 0)
    def _():
        for jb in range(MOBA_NBLK):
            kb = k_ref[jb * blk:(jb + 1) * blk, :].astype(F32)
            kmean_ref[jb:jb + 1, :] = jnp.sum(kb, axis=0, keepdims=True) * (1.0 / blk)

    q = q_ref[...]
    scale = HEAD_DIM ** -0.5
    km = kmean_ref[...]
    km_hi = km.astype(BF16)
    km_lo = (km - km_hi.astype(F32)).astype(BF16)
    gate = _qk(q, km_hi) + _qk(q, km_lo)
    bidx = lax.broadcasted_iota(jnp.int32, (blk, MOBA_NBLK), 1)
    vals = jnp.where(bidx < i, gate, NEG_INF)
    sel = jnp.zeros((blk, MOBA_NBLK), F32)
    for _ in range(MOBA_TOPK):
        mx = jnp.max(vals, axis=1, keepdims=True)
        idx = jnp.min(jnp.where(vals == mx, bidx, MOBA_NBLK), axis=1, keepdims=True)
        hit = bidx == idx
        sel = jnp.where(hit & (idx < i), 1.0, sel)
        vals = jnp.where(hit, BELOW_NEG_INF, vals)

    def scores(j):
        k = k_ref[pl.ds(pl.multiple_of(j * blk, blk), blk), :]
        return _qk(q, k) * scale

    def values(j):
        return v_ref[pl.ds(pl.multiple_of(j * blk, blk), blk), :]

    row = lax.broadcasted_iota(jnp.int32, (blk, blk), 0)
    col = lax.broadcasted_iota(jnp.int32, (blk, blk), 1)
    s = jnp.where(col <= row, scores(i), NEG_INF)
    carry = _softmax_first(s, values(i))

    def body(j, cr):
        chosen = jnp.sum(jnp.where(bidx == j, sel, 0.0), axis=1, keepdims=True)
        s = jnp.where(chosen > 0.5, scores(j), NEG_INF)
        return _softmax_update(s, values(j), cr)

    m, l, acc = lax.fori_loop(0, i, body, carry)
    g = g_ref[...].astype(F32)
    o_ref[...] = ((acc / l) * _silu(g)).astype(o_ref.dtype)


def _attention_specs(tile, seg_q, seg_k, seg_v, seg_g):
    nq = SEQ // tile

    def rows(bh, i):
        return (bh // HEADS) * nq + i

    def qspec(seg):
        return pl.BlockSpec((tile, HEAD_DIM), lambda bh, i: (rows(bh, i), seg * HEADS + bh % HEADS))

    def kvspec(seg):
        return pl.BlockSpec((SEQ, HEAD_DIM), lambda bh, i: (bh // HEADS, seg * HEADS + bh % HEADS))

    in_specs = [qspec(seg_q), kvspec(seg_k), kvspec(seg_v), qspec(seg_g)]
    out_spec = pl.BlockSpec((tile, HEAD_DIM), lambda bh, i: (rows(bh, i), bh % HEADS))
    return nq, in_specs, out_spec


def fox_attention(proj, c_blocks):
    tile = FOX_TILE
    nq, in_specs, out_spec = _attention_specs(tile, SEG_F_Q, SEG_F_K, SEG_F_V, SEG_F_GATE)
    in_specs.append(pl.BlockSpec((None, nq, 1, tile), lambda bh, i: (bh, 0, 0, 0)))
    return pl.pallas_call(
        functools.partial(_fox_kernel, tile=tile),
        grid=(BATCH * HEADS, nq),
        in_specs=in_specs,
        out_specs=out_spec,
        out_shape=jax.ShapeDtypeStruct((TOKENS, BRANCH_WIDTH), BF16),
        compiler_params=_params(2),
        name="fox_attention",
    )(proj, proj, proj, proj, c_blocks)


def moba_attention(proj):
    nq, in_specs, out_spec = _attention_specs(MOBA_BLOCK, SEG_A_Q, SEG_A_K, SEG_A_V, SEG_A_GATE)
    return pl.pallas_call(
        _moba_kernel,
        grid=(BATCH * HEADS, nq),
        in_specs=in_specs,
        out_specs=out_spec,
        out_shape=jax.ShapeDtypeStruct((TOKENS, BRANCH_WIDTH), BF16),
        scratch_shapes=[pltpu.VMEM((MOBA_NBLK, HEAD_DIM), F32)],
        compiler_params=_params(2),
        name="moba_attention",
    )(proj, proj, proj, proj)


def _shift_rows(x, d):
    return pltpu.roll(x, d, 0)


def _mixer_kernel(pin_ref, pin_h_ref, pg_ref, cb_ref, cc_ref, cc_h_ref, cx_ref, cx_h_ref, cg_ref,
                  wpool_ref, pscale_ref, wconv_ref, oc_ref, od_ref, *, tm):
    i = pl.program_id(0)
    t0 = (i * tm) % SEQ
    keep = jnp.where(t0 == 0, 0.0, 1.0)
    t_seq = t0 + lax.broadcasted_iota(jnp.int32, (tm, 1), 0)

    u = pin_ref[...].astype(F32)
    ue = jnp.concatenate([pin_h_ref[...].astype(F32) * keep, u], axis=0)
    for g, win in enumerate(POOL_WINDOWS):
        sl = slice(g * POOL_GROUP_WIDTH, (g + 1) * POOL_GROUP_WIDTH)
        a = ue[:, sl]
        d = 1
        while d < win:
            a = a + _shift_rows(a, d)
            d *= 2
        cnt = jnp.minimum(t_seq + 1, win).astype(F32)
        pooled = a[HALO:, :] / cnt - u[:, sl]
        y = jnp.dot(pooled.astype(BF16), wpool_ref[g], preferred_element_type=F32)
        y = y * pscale_ref[:, sl]
        oc_ref[:, sl] = (y * _silu(pg_ref[:, sl].astype(F32))).astype(oc_ref.dtype)

    z = cc_ref[...].astype(F32) * cx_ref[...].astype(F32)
    zh = cc_h_ref[...].astype(F32) * cx_h_ref[...].astype(F32) * keep
    ze = jnp.concatenate([zh, z], axis=0)
    y = (wconv_ref[2:3, :] * z
         + wconv_ref[1:2, :] * _shift_rows(ze, 1)[HALO:, :]
         + wconv_ref[0:1, :] * _shift_rows(ze, 2)[HALO:, :])
    od = cb_ref[...].astype(F32) * y
    od_ref[...] = (od * _silu(cg_ref[...].astype(F32))).astype(od_ref.dtype)


def pool_conv_mixers(proj, w_pool, pool_scale, w_conv, tm=512):
    halo_per_tile = tm // HALO

    def main(seg):
        return pl.BlockSpec((tm, BRANCH_WIDTH), lambda i: (i, seg))

    def halo(seg):
        return pl.BlockSpec((HALO, BRANCH_WIDTH),
                            lambda i: (jnp.maximum(i * halo_per_tile - 1, 0), seg))

    const2 = lambda i: (0, 0)
    out = jax.ShapeDtypeStruct((TOKENS, BRANCH_WIDTH), BF16)
    return pl.pallas_call(
        functools.partial(_mixer_kernel, tm=tm),
        grid=(TOKENS // tm,),
        in_specs=[main(SEG_P_IN), halo(SEG_P_IN), main(SEG_P_GATE), main(SEG_C_B),
                  main(SEG_C_C), halo(SEG_C_C), main(SEG_C_X), halo(SEG_C_X), main(SEG_C_GATE),
                  pl.BlockSpec((len(POOL_WINDOWS), POOL_GROUP_WIDTH, POOL_GROUP_WIDTH),
                               lambda i: (0, 0, 0)),
                  pl.BlockSpec((1, BRANCH_WIDTH), const2),
                  pl.BlockSpec((CONV_K, BRANCH_WIDTH), const2)],
        out_specs=[pl.BlockSpec((tm, BRANCH_WIDTH), lambda i: (i, 0))] * 2,
        out_shape=[out, out],
        compiler_params=_params(1),
        name="pool_conv_mixers",
    )(proj, proj, proj, proj, proj, proj, proj, proj, proj,
      w_pool, pool_scale.reshape(1, BRANCH_WIDTH), w_conv)


def _merge_kernel(hn_ref, wmg_ref, bm_ref, br_ref, wbr_ref, o_ref, acc_ref):
    k = pl.program_id(2)
    gate = jnp.dot(hn_ref[...], wmg_ref[...], preferred_element_type=F32) + bm_ref[...]
    up = jnp.dot(br_ref[...], wbr_ref[...], preferred_element_type=F32)
    contrib = _sigmoid(gate) * up

    @pl.when(k == 0)
    def _():
        acc_ref[...] = contrib

    @pl.when(k > 0)
    def _():
        acc_ref[...] += contrib

    @pl.when(k == N_BRANCH - 1)
    def _():
        o_ref[...] = acc_ref[...].astype(o_ref.dtype)


def gated_merge(hn, w_mg, b_merge, branches, w_branch, tm=1024, tn=512):
    nj = D_MODEL // tn
    return pl.pallas_call(
        _merge_kernel,
        grid=(TOKENS // tm, nj, N_BRANCH),
        in_specs=[pl.BlockSpec((tm, D_MODEL), lambda i, j, k: (i, 0)),
                  pl.BlockSpec((D_MODEL, tn), lambda i, j, k: (0, k * nj + j)),
                  pl.BlockSpec((None, 1, tn), lambda i, j, k: (k, 0, j)),
                  pl.BlockSpec((tm, BRANCH_WIDTH), lambda i, j, k: (i, k)),
                  pl.BlockSpec((None, BRANCH_WIDTH, tn), lambda i, j, k: (k, 0, j))],
        out_specs=pl.BlockSpec((tm, tn), lambda i, j, k: (i, j)),
        out_shape=jax.ShapeDtypeStruct((TOKENS, D_MODEL), BF16),
        scratch_shapes=[pltpu.VMEM((tm, tn), F32)],
        compiler_params=_params(3),
        name="gated_merge",
    )(hn, w_mg, b_merge.reshape(N_BRANCH, 1, D_MODEL), branches, w_branch)


def _outproj_kernel(m_ref, w_ref, x_ref, g_ref, *refs, tn, final):
    if final:
        y_ref, row_ref = refs[0], refs[0]
    else:
        h_ref, y_ref, row_ref = refs
    j = pl.program_id(1)
    nj = D_MODEL // tn
    h = x_ref[...] + jnp.dot(m_ref[...], w_ref[...], preferred_element_type=F32)
    if not final:
        h_ref[...] = h
    for jj in range(nj):
        @pl.when(j == jj)
        def _():
            row_ref[:, jj * tn:(jj + 1) * tn] = h

    @pl.when(j == nj - 1)
    def _():
        r = row_ref[...]
        ms = jnp.mean(r * r, axis=-1, keepdims=True)
        y_ref[...] = ((r * lax.rsqrt(ms + RMS_EPS)) * g_ref[...]).astype(y_ref.dtype)


def output_projection(merged, w_out, x, g_next, final, tm=512, tn=512):
    row_spec = pl.BlockSpec((tm, D_MODEL), lambda i, j: (i, 0))
    tile_spec = pl.BlockSpec((tm, tn), lambda i, j: (i, j))
    if final:
        out_specs = row_spec
        out_shape = jax.ShapeDtypeStruct((TOKENS, D_MODEL), F32)
        scratch = []
    else:
        out_specs = [tile_spec, row_spec]
        out_shape = [jax.ShapeDtypeStruct((TOKENS, D_MODEL), F32),
                     jax.ShapeDtypeStruct((TOKENS, D_MODEL), BF16)]
        scratch = [pltpu.VMEM((tm, D_MODEL), F32)]
    return pl.pallas_call(
        functools.partial(_outproj_kernel, tn=tn, final=final),
        grid=(TOKENS // tm, D_MODEL // tn),
        in_specs=[row_spec,
                  pl.BlockSpec((D_MODEL, tn), lambda i, j: (0, j)),
                  tile_spec,
                  pl.BlockSpec((1, D_MODEL), lambda i, j: (0, 0))],
        out_specs=out_specs,
        out_shape=out_shape,
        scratch_shapes=scratch,
        compiler_params=_params(2),
        name="output_projection_final" if final else "output_projection",
    )(merged, w_out, x, g_next.reshape(1, D_MODEL))


def _layer(h, hn, w_in, b_forget, w_pool, pool_scale, w_conv, w_branch, b_merge, w_out,
           g_next, final):
    w_a = jnp.concatenate([w_in[:, :FORGET_OFF], w_in[:, FORGET_OFF + HEADS:MERGE_OFF]],
                          axis=1).astype(BF16)
    w_f = jnp.pad(w_in[:, FORGET_OFF:FORGET_OFF + HEADS], ((0, 0), (0, LANES - HEADS))).astype(BF16)
    b_f = jnp.pad(b_forget, (0, LANES - HEADS)).reshape(1, LANES)
    w_mg = w_in[:, MERGE_OFF:].astype(BF16)

    proj = branch_projection(hn, w_a)
    c = forget_cumsum(hn, w_f, b_f)
    nq = SEQ // FOX_TILE
    c_blocks = (c[:, :HEADS].reshape(BATCH, SEQ, HEADS).transpose(0, 2, 1)
                .reshape(BATCH * HEADS, nq, 1, FOX_TILE))
    br_a = moba_attention(proj)
    br_b = fox_attention(proj, c_blocks)
    br_c, br_d = pool_conv_mixers(proj, w_pool.astype(BF16), pool_scale, w_conv)
    branches = jnp.concatenate([br_a, br_b, br_c, br_d], axis=1)
    merged = gated_merge(hn, w_mg, b_merge, branches, w_branch.astype(BF16))
    return output_projection(merged, w_out.astype(BF16), h, g_next, final)


def kernel(x, norm_g, w_in, b_forget, w_pool, pool_scale, w_conv, w_branch, b_merge, w_out, final_g):
    h = x.reshape(TOKENS, D_MODEL)
    hn = rmsnorm(h, norm_g[0])
    h, hn = _layer(h, hn, w_in[0], b_forget[0], w_pool[0], pool_scale[0], w_conv[0], w_branch[0],
                   b_merge[0], w_out[0], norm_g[1], final=False)
    y = _layer(h, hn, w_in[1], b_forget[1], w_pool[1], pool_scale[1], w_conv[1], w_branch[1],
               b_merge[1], w_out[1], final_g, final=True)
    return y.reshape(BATCH, SEQ, D_MODEL)
```

```python
import functools
import math

import jax
import jax.numpy as jnp
from jax import lax
from jax.experimental import pallas as pl
from jax.experimental.pallas import tpu as pltpu

D_MODEL = 4096
BATCH = 2
SEQ = 8192
DEPTH = 2
TOKENS = BATCH * SEQ
N_BRANCH = 4
BRANCH_WIDTH = D_MODEL // 4
HEAD_DIM = 128
HEADS = BRANCH_WIDTH // HEAD_DIM
MOBA_BLOCK = 256
MOBA_TOPK = 3
MOBA_NBLK = SEQ // MOBA_BLOCK
POOL_WINDOWS = (2, 4, 8, 16)
POOL_GROUP_WIDTH = BRANCH_WIDTH // len(POOL_WINDOWS)
CONV_K = 3
RMS_EPS = 1e-6
NEG_INF = -1e30
BELOW_NEG_INF = -3e38
HALO = 16
LANES = 128

ATT_SUB = MOBA_BLOCK
ATT_NSUB = 8
ATT_TQ = ATT_SUB * ATT_NSUB
ATT_NKV = SEQ // ATT_SUB

SEG_A_Q, SEG_A_K, SEG_A_V, SEG_A_GATE = 0, 1, 2, 3
SEG_F_Q, SEG_F_K, SEG_F_V, SEG_F_GATE = 4, 5, 6, 7
SEG_P_IN, SEG_P_GATE, SEG_C_B, SEG_C_C, SEG_C_X, SEG_C_GATE = 8, 9, 10, 11, 12, 13
N_SEG = 14
FORGET_OFF = 8 * BRANCH_WIDTH
PACKED_MERGE_OFF = N_SEG * BRANCH_WIDTH
PACKED_WIDTH = PACKED_MERGE_OFF + N_BRANCH * D_MODEL

LOG2E = math.log2(math.e)
Q_FOLD = HEAD_DIM ** -0.5 * LOG2E

V7X_VMEM_LIMIT = 56 * 1024 * 1024

BF16 = jnp.bfloat16
F32 = jnp.float32


def _params(n_grid):
    return pltpu.CompilerParams(dimension_semantics=("arbitrary",) * n_grid,
                                vmem_limit_bytes=V7X_VMEM_LIMIT)


def _sigmoid(x):
    return 1.0 / (1.0 + jnp.exp(-x))


def _silu(x):
    return x * _sigmoid(x)


def _rmsnorm_kernel(x_ref, g_ref, o_ref):
    x = x_ref[...]
    ms = jnp.mean(x * x, axis=-1, keepdims=True)
    o_ref[...] = ((x * lax.rsqrt(ms + RMS_EPS)) * g_ref[...]).astype(o_ref.dtype)


def rmsnorm(x, g, tm=512):
    return pl.pallas_call(
        _rmsnorm_kernel,
        grid=(TOKENS // tm,),
        in_specs=[pl.BlockSpec((tm, D_MODEL), lambda i: (i, 0)),
                  pl.BlockSpec((1, D_MODEL), lambda i: (0, 0))],
        out_specs=pl.BlockSpec((tm, D_MODEL), lambda i: (i, 0)),
        out_shape=jax.ShapeDtypeStruct((TOKENS, D_MODEL), BF16),
        compiler_params=_params(1),
        name="rmsnorm",
    )(x, g.reshape(1, D_MODEL))


def _matmul_kernel(x_ref, w_ref, o_ref):
    o_ref[...] = jnp.dot(x_ref[...], w_ref[...],
                         preferred_element_type=F32).astype(o_ref.dtype)


def branch_projection(hn, w_packed, layer, tm=1024, tn=1024):
    n = PACKED_MERGE_OFF
    return pl.pallas_call(
        _matmul_kernel,
        grid=(TOKENS // tm, n // tn),
        in_specs=[pl.BlockSpec((tm, D_MODEL), lambda i, j: (i, 0)),
                  pl.BlockSpec((None, D_MODEL, tn), lambda i, j: (layer, 0, j))],
        out_specs=pl.BlockSpec((tm, tn), lambda i, j: (i, j)),
        out_shape=jax.ShapeDtypeStruct((TOKENS, n), BF16),
        compiler_params=_params(2),
        name="branch_projection",
    )(hn, w_packed)


def _forget_kernel(hn_ref, w_ref, b_ref, c_ref, carry_ref, *, tm):
    i = pl.program_id(0)

    @pl.when((i * tm) % SEQ == 0)
    def _():
        carry_ref[...] = jnp.zeros_like(carry_ref)

    z = jnp.dot(hn_ref[...], w_ref[...], preferred_element_type=F32) + b_ref[...]
    logf = jnp.minimum(z, 0.0) - jnp.log(1.0 + jnp.exp(-jnp.abs(z)))
    r = lax.broadcasted_iota(jnp.int32, (tm, tm), 0)
    c = lax.broadcasted_iota(jnp.int32, (tm, tm), 1)
    tri = (c <= r).astype(BF16)
    hi = logf.astype(BF16)
    rem = logf - hi.astype(F32)
    mid = rem.astype(BF16)
    lo = (rem - mid.astype(F32)).astype(BF16)
    cum = (jnp.dot(tri, hi, preferred_element_type=F32)
           + jnp.dot(tri, mid, preferred_element_type=F32)
           + jnp.dot(tri, lo, preferred_element_type=F32))
    out = cum + carry_ref[...]
    c_ref[...] = out
    carry_ref[...] = out[tm - 1:tm, :]


def forget_cumsum(hn, wf, bf, tm=1024):
    return pl.pallas_call(
        functools.partial(_forget_kernel, tm=tm),
        grid=(TOKENS // tm,),
        in_specs=[pl.BlockSpec((tm, D_MODEL), lambda i: (i, 0)),
                  pl.BlockSpec((D_MODEL, LANES), lambda i: (0, 0)),
                  pl.BlockSpec((1, LANES), lambda i: (0, 0))],
        out_specs=pl.BlockSpec((tm, LANES), lambda i: (i, 0)),
        out_shape=jax.ShapeDtypeStruct((TOKENS, LANES), F32),
        scratch_shapes=[pltpu.VMEM((1, LANES), F32)],
        compiler_params=_params(1),
        name="forget_cumsum",
    )(hn, wf, bf)


def _scores_t(k, q):
    return lax.dot_general(k, q, (((1,), (1,)), ((), ())), preferred_element_type=F32)


def _flash_step(s_t, v_t, m, l, acc_ref, cols):
    m_new = jnp.maximum(m, jnp.max(s_t, axis=0, keepdims=True))
    alpha = jnp.exp2(m - m_new)
    p = jnp.exp2(s_t - m_new)
    l_new = alpha * l + jnp.sum(p, axis=0, keepdims=True)
    acc_ref[:, cols] = alpha * acc_ref[:, cols] + jnp.dot(v_t, p.astype(BF16),
                                                          preferred_element_type=F32)
    return m_new, l_new


def _fill_v_transposed(v_ref, vt_ref):
    for jb in range(ATT_NKV):
        vt_ref[jb] = v_ref[jb * ATT_SUB:(jb + 1) * ATT_SUB, :].T


def _attend(q_ref, k_ref, vt_ref, g_ref, o_ref, acc_ref, i, bias_fn, past_mask_fn):
    sub = ATT_SUB
    qs = [q_ref[r * sub:(r + 1) * sub, :] for r in range(ATT_NSUB)]
    cols = [slice(r * sub, (r + 1) * sub) for r in range(ATT_NSUB)]
    acc_ref[...] = jnp.zeros_like(acc_ref)
    causal = (lax.broadcasted_iota(jnp.int32, (sub, sub), 0)
              <= lax.broadcasted_iota(jnp.int32, (sub, sub), 1))

    def load(j):
        rows = pl.ds(pl.multiple_of(j * sub, sub), sub)
        return k_ref[rows, :], vt_ref[j], bias_fn(rows)

    def block_steps(j, chains, carries, diagonal_chain):
        kb, vt, bias = load(j)
        scores = {r: _scores_t(kb, qs[r]) for r in chains}
        carries = list(carries)
        for r in chains:
            s_t = scores[r] if bias is None else scores[r] - bias
            mask = causal if r == diagonal_chain else past_mask_fn(r, j)
            if mask is not None:
                s_t = jnp.where(mask, s_t, NEG_INF)
            carries[r] = _flash_step(s_t, vt, *carries[r], acc_ref, cols[r])
        return tuple(carries)

    base = i * ATT_NSUB
    carries = ((jnp.full((1, sub), NEG_INF, F32), jnp.zeros((1, sub), F32)),) * ATT_NSUB
    for jj in range(ATT_NSUB):
        carries = block_steps(base + jj, range(jj, ATT_NSUB), carries, jj)

    carries = lax.fori_loop(
        0, base, lambda j, cr: block_steps(j, range(ATT_NSUB), cr, None), carries)
    for r in range(ATT_NSUB):
        o = (acc_ref[:, cols[r]] / carries[r][1]).T
        g = g_ref[cols[r], :].astype(F32)
        o_ref[cols[r], :] = (o * _silu(g)).astype(o_ref.dtype)


def _fox_kernel(q_ref, k_ref, v_ref, g_ref, c_ref, alias_ref, o_ref, vt_ref, cb_ref, acc_ref):
    del alias_ref
    bh = pl.program_id(0)
    i = pl.program_id(1)
    chunk = 512

    @pl.when(i == 0)
    def _():
        _fill_v_transposed(v_ref, vt_ref)
        head = bh % HEADS
        lane = lax.broadcasted_iota(jnp.int32, (chunk, LANES), 1)

        def fill(ch, carry):
            rows = pl.ds(pl.multiple_of(ch * chunk, chunk), chunk)
            col = jnp.sum(jnp.where(lane == head, c_ref[rows, :], 0.0), axis=1, keepdims=True)
            cb_ref[rows, :] = jnp.broadcast_to(col * LOG2E, (chunk, LANES))
            return carry

        lax.fori_loop(0, SEQ // chunk, fill, 0)

    def bias_fn(rows):
        return jnp.concatenate([cb_ref[rows, :]] * (ATT_SUB // LANES), axis=1)

    _attend(q_ref, k_ref, vt_ref, g_ref, o_ref, acc_ref, i, bias_fn, lambda r, j: None)


def _moba_kernel(q_ref, k_ref, v_ref, g_ref, o_ref, vt_ref, kmean_ref, sel_ref, acc_ref):
    i = pl.program_id(1)
    blk = MOBA_BLOCK

    @pl.when(i == 0)
    def _():
        _fill_v_transposed(v_ref, vt_ref)
        for jb in range(MOBA_NBLK):
            kb = k_ref[jb * blk:(jb + 1) * blk, :].astype(F32)
            kmean_ref[jb:jb + 1, :] = jnp.sum(kb, axis=0, keepdims=True) * (1.0 / blk)

    q = q_ref[...]
    km = kmean_ref[...]
    km_hi = km.astype(BF16)
    km_lo = (km - km_hi.astype(F32)).astype(BF16)
    gate = _scores_t(km_hi, q) + _scores_t(km_lo, q)
    bidx = lax.broadcasted_iota(jnp.int32, (MOBA_NBLK, ATT_TQ), 0)
    qblk = i * ATT_NSUB + lax.broadcasted_iota(jnp.int32, (MOBA_NBLK, ATT_TQ), 1) // blk
    vals = jnp.where(bidx < qblk, gate, NEG_INF)
    sel = jnp.zeros((MOBA_NBLK, ATT_TQ), F32)
    for _ in range(MOBA_TOPK):
        mx = jnp.max(vals, axis=0, keepdims=True)
        idx = jnp.min(jnp.where(vals == mx, bidx, MOBA_NBLK), axis=0, keepdims=True)
        hit = bidx == idx
        sel = jnp.where(hit & (idx < qblk), 1.0, sel)
        vals = jnp.where(hit, BELOW_NEG_INF, vals)
    sel_ref[...] = sel

    def chosen(r, j):
        return sel_ref[pl.ds(j, 1), r * ATT_SUB:(r + 1) * ATT_SUB] > 0.5

    _attend(q_ref, k_ref, vt_ref, g_ref, o_ref, acc_ref, i, lambda rows: None, chosen)


def _attention_specs(seg_q, seg_k, seg_v, seg_g, out_seg):
    nq = SEQ // ATT_TQ

    def rows(bh, i):
        return (bh // HEADS) * nq + i

    def qspec(seg):
        return pl.BlockSpec((ATT_TQ, HEAD_DIM),
                            lambda bh, i: (rows(bh, i), seg * HEADS + bh % HEADS))

    def kvspec(seg):
        return pl.BlockSpec((SEQ, HEAD_DIM), lambda bh, i: (bh // HEADS, seg * HEADS + bh % HEADS))

    in_specs = [qspec(seg_q), kvspec(seg_k), kvspec(seg_v), qspec(seg_g)]
    return (BATCH * HEADS, nq), in_specs, qspec(out_seg)


BRANCHES_SHAPE = jax.ShapeDtypeStruct((TOKENS, D_MODEL), BF16)
VT_SCRATCH = pltpu.VMEM((ATT_NKV, HEAD_DIM, ATT_SUB), BF16)
ACC_SCRATCH = pltpu.VMEM((HEAD_DIM, ATT_TQ), F32)


def moba_attention(proj):
    grid, in_specs, out_spec = _attention_specs(SEG_A_Q, SEG_A_K, SEG_A_V, SEG_A_GATE, 0)
    return pl.pallas_call(
        _moba_kernel,
        grid=grid,
        in_specs=in_specs,
        out_specs=out_spec,
        out_shape=BRANCHES_SHAPE,
        scratch_shapes=[VT_SCRATCH,
                        pltpu.VMEM((MOBA_NBLK, HEAD_DIM), F32),
                        pltpu.VMEM((MOBA_NBLK, ATT_TQ), F32),
                        ACC_SCRATCH],
        compiler_params=_params(2),
        name="moba_attention",
    )(proj, proj, proj, proj)


def fox_attention(proj, c, branches):
    grid, in_specs, out_spec = _attention_specs(SEG_F_Q, SEG_F_K, SEG_F_V, SEG_F_GATE, 1)
    in_specs.append(pl.BlockSpec((SEQ, LANES), lambda bh, i: (bh // HEADS, 0)))
    in_specs.append(pl.BlockSpec(memory_space=pl.ANY))
    return pl.pallas_call(
        _fox_kernel,
        grid=grid,
        in_specs=in_specs,
        out_specs=out_spec,
        out_shape=BRANCHES_SHAPE,
        scratch_shapes=[VT_SCRATCH, pltpu.VMEM((SEQ, LANES), F32), ACC_SCRATCH],
        input_output_aliases={5: 0},
        compiler_params=_params(2),
        name="fox_attention",
    )(proj, proj, proj, proj, c, branches)


def _shift_rows(x, d):
    return pltpu.roll(x, d, 0)


def _mixer_kernel(pin_ref, pin_h_ref, pg_ref, cb_ref, cc_ref, cc_h_ref, cx_ref, cx_h_ref, cg_ref,
                  wpool_ref, pscale_ref, wconv_ref, alias_ref, o_ref, *, tm):
    del alias_ref
    i = pl.program_id(0)
    t0 = (i * tm) % SEQ
    keep = jnp.where(t0 == 0, 0.0, 1.0)
    t_seq = t0 + lax.broadcasted_iota(jnp.int32, (tm, 1), 0)

    u = pin_ref[...].astype(F32)
    ue = jnp.concatenate([pin_h_ref[...].astype(F32) * keep, u], axis=0)
    for g, win in enumerate(POOL_WINDOWS):
        sl = slice(g * POOL_GROUP_WIDTH, (g + 1) * POOL_GROUP_WIDTH)
        a = ue[:, sl]
        d = 1
        while d < win:
            a = a + _shift_rows(a, d)
            d *= 2
        cnt = jnp.minimum(t_seq + 1, win).astype(F32)
        pooled = a[HALO:, :] / cnt - u[:, sl]
        y = jnp.dot(pooled.astype(BF16), wpool_ref[g], preferred_element_type=F32)
        y = y * pscale_ref[:, sl]
        o_ref[:, sl] = (y * _silu(pg_ref[:, sl].astype(F32))).astype(o_ref.dtype)

    z = cc_ref[...].astype(F32) * cx_ref[...].astype(F32)
    zh = cc_h_ref[...].astype(F32) * cx_h_ref[...].astype(F32) * keep
    ze = jnp.concatenate([zh, z], axis=0)
    y = (wconv_ref[2:3, :] * z
         + wconv_ref[1:2, :] * _shift_rows(ze, 1)[HALO:, :]
         + wconv_ref[0:1, :] * _shift_rows(ze, 2)[HALO:, :])
    od = cb_ref[...].astype(F32) * y
    o_ref[:, BRANCH_WIDTH:] = (od * _silu(cg_ref[...].astype(F32))).astype(o_ref.dtype)


def pool_conv_mixers(proj, w_pool, pool_scale, w_conv, branches, tm=512):
    halo_per_tile = tm // HALO

    def main(seg):
        return pl.BlockSpec((tm, BRANCH_WIDTH), lambda i: (i, seg))

    def halo(seg):
        return pl.BlockSpec((HALO, BRANCH_WIDTH),
                            lambda i: (jnp.maximum(i * halo_per_tile - 1, 0), seg))

    const2 = lambda i: (0, 0)
    return pl.pallas_call(
        functools.partial(_mixer_kernel, tm=tm),
        grid=(TOKENS // tm,),
        in_specs=[main(SEG_P_IN), halo(SEG_P_IN), main(SEG_P_GATE), main(SEG_C_B),
                  main(SEG_C_C), halo(SEG_C_C), main(SEG_C_X), halo(SEG_C_X), main(SEG_C_GATE),
                  pl.BlockSpec((len(POOL_WINDOWS), POOL_GROUP_WIDTH, POOL_GROUP_WIDTH),
                               lambda i: (0, 0, 0)),
                  pl.BlockSpec((1, BRANCH_WIDTH), const2),
                  pl.BlockSpec((CONV_K, BRANCH_WIDTH), const2),
                  pl.BlockSpec(memory_space=pl.ANY)],
        out_specs=pl.BlockSpec((tm, 2 * BRANCH_WIDTH), lambda i: (i, 1)),
        out_shape=BRANCHES_SHAPE,
        input_output_aliases={12: 0},
        compiler_params=_params(1),
        name="pool_conv_mixers",
    )(proj, proj, proj, proj, proj, proj, proj, proj, proj,
      w_pool, pool_scale.reshape(1, BRANCH_WIDTH), w_conv, branches)


def _merge_kernel(hn_ref, wmg_ref, bm_ref, br_ref, wbr_ref, o_ref, acc_ref):
    k = pl.program_id(2)
    gate = jnp.dot(hn_ref[...], wmg_ref[...], preferred_element_type=F32) + bm_ref[...]
    up = jnp.dot(br_ref[...], wbr_ref[...], preferred_element_type=F32)
    contrib = _sigmoid(gate) * up

    @pl.when(k == 0)
    def _():
        acc_ref[...] = contrib

    @pl.when(k > 0)
    def _():
        acc_ref[...] += contrib

    @pl.when(k == N_BRANCH - 1)
    def _():
        o_ref[...] = acc_ref[...].astype(o_ref.dtype)


def gated_merge(hn, w_packed, layer, b_merge, branches, w_branch, tm=1024, tn=512):
    nj = D_MODEL // tn
    mg0 = PACKED_MERGE_OFF // tn
    return pl.pallas_call(
        _merge_kernel,
        grid=(TOKENS // tm, nj, N_BRANCH),
        in_specs=[pl.BlockSpec((tm, D_MODEL), lambda i, j, k: (i, 0)),
                  pl.BlockSpec((None, D_MODEL, tn), lambda i, j, k: (layer, 0, mg0 + k * nj + j)),
                  pl.BlockSpec((None, 1, tn), lambda i, j, k: (k, 0, j)),
                  pl.BlockSpec((tm, BRANCH_WIDTH), lambda i, j, k: (i, k)),
                  pl.BlockSpec((None, BRANCH_WIDTH, tn), lambda i, j, k: (k, 0, j))],
        out_specs=pl.BlockSpec((tm, tn), lambda i, j, k: (i, j)),
        out_shape=jax.ShapeDtypeStruct((TOKENS, D_MODEL), BF16),
        scratch_shapes=[pltpu.VMEM((tm, tn), F32)],
        compiler_params=_params(3),
        name="gated_merge",
    )(hn, w_packed, b_merge.reshape(N_BRANCH, 1, D_MODEL), branches, w_branch)


def _outproj_kernel(m_ref, w_ref, x_ref, g_ref, *refs, tn, final):
    if final:
        y_ref, row_ref = refs[0], refs[0]
    else:
        h_ref, y_ref, row_ref = refs
    j = pl.program_id(1)
    nj = D_MODEL // tn
    h = x_ref[...] + jnp.dot(m_ref[...], w_ref[...], preferred_element_type=F32)
    if not final:
        h_ref[...] = h
    for jj in range(nj):
        @pl.when(j == jj)
        def _():
            row_ref[:, jj * tn:(jj + 1) * tn] = h

    @pl.when(j == nj - 1)
    def _():
        r = row_ref[...]
        ms = jnp.mean(r * r, axis=-1, keepdims=True)
        y_ref[...] = ((r * lax.rsqrt(ms + RMS_EPS)) * g_ref[...]).astype(y_ref.dtype)


def output_projection(merged, w_out, x, g_next, final, tm=512, tn=512):
    row_spec = pl.BlockSpec((tm, D_MODEL), lambda i, j: (i, 0))
    tile_spec = pl.BlockSpec((tm, tn), lambda i, j: (i, j))
    if final:
        out_specs = row_spec
        out_shape = jax.ShapeDtypeStruct((TOKENS, D_MODEL), F32)
        scratch = []
    else:
        out_specs = [tile_spec, row_spec]
        out_shape = [jax.ShapeDtypeStruct((TOKENS, D_MODEL), F32),
                     jax.ShapeDtypeStruct((TOKENS, D_MODEL), BF16)]
        scratch = [pltpu.VMEM((tm, D_MODEL), F32)]
    return pl.pallas_call(
        functools.partial(_outproj_kernel, tn=tn, final=final),
        grid=(TOKENS // tm, D_MODEL // tn),
        in_specs=[row_spec,
                  pl.BlockSpec((D_MODEL, tn), lambda i, j: (0, j)),
                  tile_spec,
                  pl.BlockSpec((1, D_MODEL), lambda i, j: (0, 0))],
        out_specs=out_specs,
        out_shape=out_shape,
        scratch_shapes=scratch,
        compiler_params=_params(2),
        name="output_projection_final" if final else "output_projection",
    )(merged, w_out, x, g_next.reshape(1, D_MODEL))


def _pack_input_weights(w_in):
    fold = jnp.ones((PACKED_WIDTH,), F32)
    for seg in (SEG_A_Q, SEG_F_Q):
        fold = fold.at[seg * BRANCH_WIDTH:(seg + 1) * BRANCH_WIDTH].set(Q_FOLD)
    packed = jnp.concatenate([w_in[:, :, :FORGET_OFF], w_in[:, :, FORGET_OFF + HEADS:]], axis=2)
    return (packed * fold).astype(BF16)


def _layer(layer, h, hn, w_packed, w_f, b_f, w_pool, pool_scale, w_conv, w_branch, b_merge, w_out,
           g_next, final):
    proj = branch_projection(hn, w_packed, layer)
    c = forget_cumsum(hn, w_f, b_f)
    branches = moba_attention(proj)
    branches = fox_attention(proj, c, branches)
    branches = pool_conv_mixers(proj, w_pool, pool_scale, w_conv, branches)
    merged = gated_merge(hn, w_packed, layer, b_merge, branches, w_branch)
    return output_projection(merged, w_out, h, g_next, final)


def kernel(x, norm_g, w_in, b_forget, w_pool, pool_scale, w_conv, w_branch, b_merge, w_out, final_g):
    w_packed = _pack_input_weights(w_in)
    w_f = jnp.pad(w_in[:, :, FORGET_OFF:FORGET_OFF + HEADS],
                  ((0, 0), (0, 0), (0, LANES - HEADS))).astype(BF16)
    b_f = jnp.pad(b_forget, ((0, 0), (0, LANES - HEADS))).reshape(DEPTH, 1, LANES)
    w_pool_b = w_pool.astype(BF16)
    w_branch_b = w_branch.astype(BF16)
    w_out_b = w_out.astype(BF16)
    g_next = (norm_g[1], final_g)

    h = x.reshape(TOKENS, D_MODEL)
    hn = rmsnorm(h, norm_g[0])
    for layer in range(DEPTH):
        final = layer == DEPTH - 1
        out = _layer(layer, h, hn, w_packed, w_f[layer], b_f[layer], w_pool_b[layer],
                     pool_scale[layer], w_conv[layer], w_branch_b[layer], b_merge[layer],
                     w_out_b[layer], g_next[layer], final)
        if final:
            return out.reshape(BATCH, SEQ, D_MODEL)
        h, hn = out
```

```python
import functools
import math

import jax
import jax.numpy as jnp
from jax import lax
from jax.experimental import pallas as pl
from jax.experimental.pallas import tpu as pltpu

D_MODEL = 4096
BATCH = 2
SEQ = 8192
DEPTH = 2
TOKENS = BATCH * SEQ
N_BRANCH = 4
BRANCH_WIDTH = D_MODEL // 4
HEAD_DIM = 128
HEADS = BRANCH_WIDTH // HEAD_DIM
MOBA_BLOCK = 256
MOBA_TOPK = 3
MOBA_NBLK = SEQ // MOBA_BLOCK
POOL_WINDOWS = (2, 4, 8, 16)
POOL_GROUP_WIDTH = BRANCH_WIDTH // len(POOL_WINDOWS)
CONV_K = 3
RMS_EPS = 1e-6
NEG_INF = -1e30
BELOW_NEG_INF = -3e38
HALO = 16
LANES = 128

ATT_SUB = MOBA_BLOCK
ATT_NSUB = 8
ATT_TQ = ATT_SUB * ATT_NSUB
ATT_NKV = SEQ // ATT_SUB

SEG_A_Q, SEG_A_K, SEG_A_V, SEG_A_GATE = 0, 1, 2, 3
SEG_F_Q, SEG_F_K, SEG_F_V, SEG_F_GATE = 4, 5, 6, 7
SEG_P_IN, SEG_P_GATE, SEG_C_B, SEG_C_C, SEG_C_X, SEG_C_GATE = 8, 9, 10, 11, 12, 13
N_SEG = 14
FORGET_OFF = 8 * BRANCH_WIDTH
PACKED_MERGE_OFF = N_SEG * BRANCH_WIDTH
PACKED_WIDTH = PACKED_MERGE_OFF + N_BRANCH * D_MODEL

LOG2E = math.log2(math.e)
Q_FOLD = HEAD_DIM ** -0.5 * LOG2E

V7X_VMEM_LIMIT = 56 * 1024 * 1024

BF16 = jnp.bfloat16
F32 = jnp.float32


def _params(n_grid):
    return pltpu.CompilerParams(dimension_semantics=("arbitrary",) * n_grid,
                                vmem_limit_bytes=V7X_VMEM_LIMIT)


def _sigmoid(x):
    return 1.0 / (1.0 + jnp.exp(-x))


def _silu(x):
    return x * _sigmoid(x)


def _rmsnorm_kernel(x_ref, g_ref, o_ref):
    x = x_ref[...]
    ms = jnp.mean(x * x, axis=-1, keepdims=True)
    o_ref[...] = ((x * lax.rsqrt(ms + RMS_EPS)) * g_ref[...]).astype(o_ref.dtype)


def rmsnorm(x, g, tm=512):
    return pl.pallas_call(
        _rmsnorm_kernel,
        grid=(TOKENS // tm,),
        in_specs=[pl.BlockSpec((tm, D_MODEL), lambda i: (i, 0)),
                  pl.BlockSpec((1, D_MODEL), lambda i: (0, 0))],
        out_specs=pl.BlockSpec((tm, D_MODEL), lambda i: (i, 0)),
        out_shape=jax.ShapeDtypeStruct((TOKENS, D_MODEL), BF16),
        compiler_params=_params(1),
        name="rmsnorm",
    )(x, g.reshape(1, D_MODEL))


def _matmul_kernel(x_ref, w_ref, o_ref):
    o_ref[...] = jnp.dot(x_ref[...], w_ref[...],
                         preferred_element_type=F32).astype(o_ref.dtype)


def branch_projection(hn, w_packed, layer, tm=1024, tn=1024):
    n = PACKED_MERGE_OFF
    return pl.pallas_call(
        _matmul_kernel,
        grid=(TOKENS // tm, n // tn),
        in_specs=[pl.BlockSpec((tm, D_MODEL), lambda i, j: (i, 0)),
                  pl.BlockSpec((None, D_MODEL, tn), lambda i, j: (layer, 0, j))],
        out_specs=pl.BlockSpec((tm, tn), lambda i, j: (i, j)),
        out_shape=jax.ShapeDtypeStruct((TOKENS, n), BF16),
        compiler_params=_params(2),
        name="branch_projection",
    )(hn, w_packed)


def _forget_kernel(hn_ref, w_ref, b_ref, c_ref, carry_ref, *, tm):
    i = pl.program_id(0)

    @pl.when((i * tm) % SEQ == 0)
    def _():
        carry_ref[...] = jnp.zeros_like(carry_ref)

    z = jnp.dot(hn_ref[...], w_ref[...], preferred_element_type=F32) + b_ref[...]
    logf = jnp.minimum(z, 0.0) - jnp.log(1.0 + jnp.exp(-jnp.abs(z)))
    r = lax.broadcasted_iota(jnp.int32, (tm, tm), 0)
    c = lax.broadcasted_iota(jnp.int32, (tm, tm), 1)
    tri = (c <= r).astype(BF16)
    hi = logf.astype(BF16)
    rem = logf - hi.astype(F32)
    mid = rem.astype(BF16)
    lo = (rem - mid.astype(F32)).astype(BF16)
    cum = (jnp.dot(tri, hi, preferred_element_type=F32)
           + jnp.dot(tri, mid, preferred_element_type=F32)
           + jnp.dot(tri, lo, preferred_element_type=F32))
    out = cum + carry_ref[...]
    c_ref[...] = out
    carry_ref[...] = out[tm - 1:tm, :]


def forget_cumsum(hn, wf, bf, tm=1024):
    return pl.pallas_call(
        functools.partial(_forget_kernel, tm=tm),
        grid=(TOKENS // tm,),
        in_specs=[pl.BlockSpec((tm, D_MODEL), lambda i: (i, 0)),
                  pl.BlockSpec((D_MODEL, LANES), lambda i: (0, 0)),
                  pl.BlockSpec((1, LANES), lambda i: (0, 0))],
        out_specs=pl.BlockSpec((tm, LANES), lambda i: (i, 0)),
        out_shape=jax.ShapeDtypeStruct((TOKENS, LANES), F32),
        scratch_shapes=[pltpu.VMEM((1, LANES), F32)],
        compiler_params=_params(1),
        name="forget_cumsum",
    )(hn, wf, bf)


def _scores_t(k, q):
    return lax.dot_general(k, q, (((1,), (1,)), ((), ())), preferred_element_type=F32)


def _flash_step(s_t, v_t, m, l, acc_ref, cols):
    m_new = jnp.maximum(m, jnp.max(s_t, axis=0, keepdims=True))
    alpha = jnp.exp2(m - m_new)
    p = jnp.exp2(s_t - m_new)
    l_new = alpha * l + jnp.sum(p, axis=0, keepdims=True)
    acc_ref[:, cols] = alpha * acc_ref[:, cols] + jnp.dot(v_t, p.astype(BF16),
                                                          preferred_element_type=F32)
    return m_new, l_new


def _fill_v_transposed(v_ref, vt_ref):
    for jb in range(ATT_NKV):
        vt_ref[jb] = v_ref[jb * ATT_SUB:(jb + 1) * ATT_SUB, :].T


def _attend(q_ref, k_ref, vt_ref, g_ref, o_ref, acc_ref, i, bias_fn, past_mask_fn):
    sub = ATT_SUB
    qs = [q_ref[r * sub:(r + 1) * sub, :] for r in range(ATT_NSUB)]
    cols = [slice(r * sub, (r + 1) * sub) for r in range(ATT_NSUB)]
    acc_ref[...] = jnp.zeros_like(acc_ref)
    causal = (lax.broadcasted_iota(jnp.int32, (sub, sub), 0)
              <= lax.broadcasted_iota(jnp.int32, (sub, sub), 1))

    def load(j):
        rows = pl.ds(pl.multiple_of(j * sub, sub), sub)
        return k_ref[rows, :], vt_ref[j], bias_fn(rows)

    def block_steps(j, chains, carries, diagonal_chain):
        kb, vt, bias = load(j)
        scores = {r: _scores_t(kb, qs[r]) for r in chains}
        carries = list(carries)
        for r in chains:
            s_t = scores[r] if bias is None else scores[r] - bias
            mask = causal if r == diagonal_chain else past_mask_fn(r, j)
            if mask is not None:
                s_t = jnp.where(mask, s_t, NEG_INF)
            carries[r] = _flash_step(s_t, vt, *carries[r], acc_ref, cols[r])
        return tuple(carries)

    base = i * ATT_NSUB
    carries = ((jnp.full((1, sub), NEG_INF, F32), jnp.zeros((1, sub), F32)),) * ATT_NSUB
    for jj in range(ATT_NSUB):
        carries = block_steps(base + jj, range(jj, ATT_NSUB), carries, jj)

    carries = lax.fori_loop(
        0, base, lambda j, cr: block_steps(j, range(ATT_NSUB), cr, None), carries)
    for r in range(ATT_NSUB):
        o = (acc_ref[:, cols[r]] / carries[r][1]).T
        g = g_ref[cols[r], :].astype(F32)
        o_ref[cols[r], :] = (o * _silu(g)).astype(o_ref.dtype)


def _fox_kernel(q_ref, k_ref, v_ref, g_ref, c_ref, alias_ref, o_ref, vt_ref, cb_ref, acc_ref):
    del alias_ref
    bh = pl.program_id(0)
    i = pl.program_id(1)
    chunk = 512

    @pl.when(i == 0)
    def _():
        _fill_v_transposed(v_ref, vt_ref)
        head = bh % HEADS
        lane = lax.broadcasted_iota(jnp.int32, (chunk, LANES), 1)

        def fill(ch, carry):
            rows = pl.ds(pl.multiple_of(ch * chunk, chunk), chunk)
            col = jnp.sum(jnp.where(lane == head, c_ref[rows, :], 0.0), axis=1, keepdims=True)
            cb_ref[rows, :] = jnp.broadcast_to(col * LOG2E, (chunk, LANES))
            return carry

        lax.fori_loop(0, SEQ // chunk, fill, 0)

    def bias_fn(rows):
        return jnp.concatenate([cb_ref[rows, :]] * (ATT_SUB // LANES), axis=1)

    _attend(q_ref, k_ref, vt_ref, g_ref, o_ref, acc_ref, i, bias_fn, lambda r, j: None)


def _moba_kernel(q_ref, k_ref, v_ref, g_ref, o_ref, vt_ref, kmean_ref, sel_ref, acc_ref):
    i = pl.program_id(1)
    blk = MOBA_BLOCK

    @pl.when(i == 0)
    def _():
        _fill_v_transposed(v_ref, vt_ref)
        for jb in range(MOBA_NBLK):
            kb = k_ref[jb * blk:(jb + 1) * blk, :].astype(F32)
            kmean_ref[jb:jb + 1, :] = jnp.sum(kb, axis=0, keepdims=True) * (1.0 / blk)

    q = q_ref[...]
    km = kmean_ref[...]
    km_hi = km.astype(BF16)
    km_lo = (km - km_hi.astype(F32)).astype(BF16)
    gate = _scores_t(km_hi, q) + _scores_t(km_lo, q)
    bidx = lax.broadcasted_iota(jnp.int32, (MOBA_NBLK, ATT_TQ), 0)
    qblk = i * ATT_NSUB + lax.broadcasted_iota(jnp.int32, (MOBA_NBLK, ATT_TQ), 1) // blk
    vals = jnp.where(bidx < qblk, gate, NEG_INF)
    sel = jnp.zeros((MOBA_NBLK, ATT_TQ), F32)
    for _ in range(MOBA_TOPK):
        mx = jnp.max(vals, axis=0, keepdims=True)
        idx = jnp.min(jnp.where(vals == mx, bidx, MOBA_NBLK), axis=0, keepdims=True)
        hit = bidx == idx
        sel = jnp.where(hit & (idx < qblk), 1.0, sel)
        vals = jnp.where(hit, BELOW_NEG_INF, vals)
    sel_ref[...] = sel

    def chosen(r, j):
        return sel_ref[pl.ds(j, 1), r * ATT_SUB:(r + 1) * ATT_SUB] > 0.5

    _attend(q_ref, k_ref, vt_ref, g_ref, o_ref, acc_ref, i, lambda rows: None, chosen)


def _attention_specs(seg_q, seg_k, seg_v, seg_g, out_seg):
    nq = SEQ // ATT_TQ

    def rows(bh, i):
        return (bh // HEADS) * nq + i

    def qspec(seg):
        return pl.BlockSpec((ATT_TQ, HEAD_DIM),
                            lambda bh, i: (rows(bh, i), seg * HEADS + bh % HEADS))

    def kvspec(seg):
        return pl.BlockSpec((SEQ, HEAD_DIM), lambda bh, i: (bh // HEADS, seg * HEADS + bh % HEADS))

    in_specs = [qspec(seg_q), kvspec(seg_k), kvspec(seg_v), qspec(seg_g)]
    return (BATCH * HEADS, nq), in_specs, qspec(out_seg)


BRANCHES_SHAPE = jax.ShapeDtypeStruct((TOKENS, D_MODEL), BF16)
VT_SCRATCH = pltpu.VMEM((ATT_NKV, HEAD_DIM, ATT_SUB), BF16)
ACC_SCRATCH = pltpu.VMEM((HEAD_DIM, ATT_TQ), F32)


def moba_attention(proj):
    grid, in_specs, out_spec = _attention_specs(SEG_A_Q, SEG_A_K, SEG_A_V, SEG_A_GATE, 0)
    return pl.pallas_call(
        _moba_kernel,
        grid=grid,
        in_specs=in_specs,
        out_specs=out_spec,
        out_shape=BRANCHES_SHAPE,
        scratch_shapes=[VT_SCRATCH,
                        pltpu.VMEM((MOBA_NBLK, HEAD_DIM), F32),
                        pltpu.VMEM((MOBA_NBLK, ATT_TQ), F32),
                        ACC_SCRATCH],
        compiler_params=_params(2),
        name="moba_attention",
    )(proj, proj, proj, proj)


def fox_attention(proj, c, branches):
    grid, in_specs, out_spec = _attention_specs(SEG_F_Q, SEG_F_K, SEG_F_V, SEG_F_GATE, 1)
    in_specs.append(pl.BlockSpec((SEQ, LANES), lambda bh, i: (bh // HEADS, 0)))
    in_specs.append(pl.BlockSpec(memory_space=pl.ANY))
    return pl.pallas_call(
        _fox_kernel,
        grid=grid,
        in_specs=in_specs,
        out_specs=out_spec,
        out_shape=BRANCHES_SHAPE,
        scratch_shapes=[VT_SCRATCH, pltpu.VMEM((SEQ, LANES), F32), ACC_SCRATCH],
        input_output_aliases={5: 0},
        compiler_params=_params(2),
        name="fox_attention",
    )(proj, proj, proj, proj, c, branches)


def _shift_rows(x, d):
    return pltpu.roll(x, d, 0)


def _mixer_kernel(pin_ref, pin_h_ref, pg_ref, cb_ref, cc_ref, cc_h_ref, cx_ref, cx_h_ref, cg_ref,
                  wpool_ref, pscale_ref, wconv_ref, alias_ref, o_ref, *, tm):
    del alias_ref
    i = pl.program_id(0)
    t0 = (i * tm) % SEQ
    keep = jnp.where(t0 == 0, 0.0, 1.0)
    t_seq = t0 + lax.broadcasted_iota(jnp.int32, (tm, 1), 0)

    u = pin_ref[...].astype(F32)
    ue = jnp.concatenate([pin_h_ref[...].astype(F32) * keep, u], axis=0)
    for g, win in enumerate(POOL_WINDOWS):
        sl = slice(g * POOL_GROUP_WIDTH, (g + 1) * POOL_GROUP_WIDTH)
        a = ue[:, sl]
        d = 1
        while d < win:
            a = a + _shift_rows(a, d)
            d *= 2
        cnt = jnp.minimum(t_seq + 1, win).astype(F32)
        pooled = a[HALO:, :] / cnt - u[:, sl]
        y = jnp.dot(pooled.astype(BF16), wpool_ref[g], preferred_element_type=F32)
        y = y * pscale_ref[:, sl]
        o_ref[:, sl] = (y * _silu(pg_ref[:, sl].astype(F32))).astype(o_ref.dtype)

    z = cc_ref[...].astype(F32) * cx_ref[...].astype(F32)
    zh = cc_h_ref[...].astype(F32) * cx_h_ref[...].astype(F32) * keep
    ze = jnp.concatenate([zh, z], axis=0)
    y = (wconv_ref[2:3, :] * z
         + wconv_ref[1:2, :] * _shift_rows(ze, 1)[HALO:, :]
         + wconv_ref[0:1, :] * _shift_rows(ze, 2)[HALO:, :])
    od = cb_ref[...].astype(F32) * y
    o_ref[:, BRANCH_WIDTH:] = (od * _silu(cg_ref[...].astype(F32))).astype(o_ref.dtype)


def pool_conv_mixers(proj, w_pool, pool_scale, w_conv, branches, tm=512):
    halo_per_tile = tm // HALO

    def main(seg):
        return pl.BlockSpec((tm, BRANCH_WIDTH), lambda i: (i, seg))

    def halo(seg):
        return pl.BlockSpec((HALO, BRANCH_WIDTH),
                            lambda i: (jnp.maximum(i * halo_per_tile - 1, 0), seg))

    const2 = lambda i: (0, 0)
    return pl.pallas_call(
        functools.partial(_mixer_kernel, tm=tm),
        grid=(TOKENS // tm,),
        in_specs=[main(SEG_P_IN), halo(SEG_P_IN), main(SEG_P_GATE), main(SEG_C_B),
                  main(SEG_C_C), halo(SEG_C_C), main(SEG_C_X), halo(SEG_C_X), main(SEG_C_GATE),
                  pl.BlockSpec((len(POOL_WINDOWS), POOL_GROUP_WIDTH, POOL_GROUP_WIDTH),
                               lambda i: (0, 0, 0)),
                  pl.BlockSpec((1, BRANCH_WIDTH), const2),
                  pl.BlockSpec((CONV_K, BRANCH_WIDTH), const2),
                  pl.BlockSpec(memory_space=pl.ANY)],
        out_specs=pl.BlockSpec((tm, 2 * BRANCH_WIDTH), lambda i: (i, 1)),
        out_shape=BRANCHES_SHAPE,
        input_output_aliases={12: 0},
        compiler_params=_params(1),
        name="pool_conv_mixers",
    )(proj, proj, proj, proj, proj, proj, proj, proj, proj,
      w_pool, pool_scale.reshape(1, BRANCH_WIDTH), w_conv, branches)


def _merge_kernel(hn_ref, wmg_ref, bm_ref, br_ref, wbr_ref, o_ref, acc_ref):
    k = pl.program_id(2)
    gate = jnp.dot(hn_ref[...], wmg_ref[...], preferred_element_type=F32) + bm_ref[...]
    up = jnp.dot(br_ref[...], wbr_ref[...], preferred_element_type=F32)
    contrib = _sigmoid(gate) * up

    @pl.when(k == 0)
    def _():
        acc_ref[...] = contrib

    @pl.when(k > 0)
    def _():
        acc_ref[...] += contrib

    @pl.when(k == N_BRANCH - 1)
    def _():
        o_ref[...] = acc_ref[...].astype(o_ref.dtype)


def gated_merge(hn, w_packed, layer, b_merge, branches, w_branch, tm=1024, tn=512):
    nj = D_MODEL // tn
    mg0 = PACKED_MERGE_OFF // tn
    return pl.pallas_call(
        _merge_kernel,
        grid=(TOKENS // tm, nj, N_BRANCH),
        in_specs=[pl.BlockSpec((tm, D_MODEL), lambda i, j, k: (i, 0)),
                  pl.BlockSpec((None, D_MODEL, tn), lambda i, j, k: (layer, 0, mg0 + k * nj + j)),
                  pl.BlockSpec((None, 1, tn), lambda i, j, k: (k, 0, j)),
                  pl.BlockSpec((tm, BRANCH_WIDTH), lambda i, j, k: (i, k)),
                  pl.BlockSpec((None, BRANCH_WIDTH, tn), lambda i, j, k: (k, 0, j))],
        out_specs=pl.BlockSpec((tm, tn), lambda i, j, k: (i, j)),
        out_shape=jax.ShapeDtypeStruct((TOKENS, D_MODEL), BF16),
        scratch_shapes=[pltpu.VMEM((tm, tn), F32)],
        compiler_params=_params(3),
        name="gated_merge",
    )(hn, w_packed, b_merge.reshape(N_BRANCH, 1, D_MODEL), branches, w_branch)


def _outproj_kernel(m_ref, w_ref, x_ref, g_ref, *refs, tn, final):
    if final:
        y_ref, row_ref = refs[0], refs[0]
    else:
        h_ref, y_ref, row_ref = refs
    j = pl.program_id(1)
    nj = D_MODEL // tn
    h = x_ref[...] + jnp.dot(m_ref[...], w_ref[...], preferred_element_type=F32)
    if not final:
        h_ref[...] = h
    for jj in range(nj):
        @pl.when(j == jj)
        def _():
            row_ref[:, jj * tn:(jj + 1) * tn] = h

    @pl.when(j == nj - 1)
    def _():
        r = row_ref[...]
        ms = jnp.mean(r * r, axis=-1, keepdims=True)
        y_ref[...] = ((r * lax.rsqrt(ms + RMS_EPS)) * g_ref[...]).astype(y_ref.dtype)


def output_projection(merged, w_out, x, g_next, final, tm=512, tn=512):
    row_spec = pl.BlockSpec((tm, D_MODEL), lambda i, j: (i, 0))
    tile_spec = pl.BlockSpec((tm, tn), lambda i, j: (i, j))
    if final:
        out_specs = row_spec
        out_shape = jax.ShapeDtypeStruct((TOKENS, D_MODEL), F32)
        scratch = []
    else:
        out_specs = [tile_spec, row_spec]
        out_shape = [jax.ShapeDtypeStruct((TOKENS, D_MODEL), F32),
                     jax.ShapeDtypeStruct((TOKENS, D_MODEL), BF16)]
        scratch = [pltpu.VMEM((tm, D_MODEL), F32)]
    return pl.pallas_call(
        functools.partial(_outproj_kernel, tn=tn, final=final),
        grid=(TOKENS // tm, D_MODEL // tn),
        in_specs=[row_spec,
                  pl.BlockSpec((D_MODEL, tn), lambda i, j: (0, j)),
                  tile_spec,
                  pl.BlockSpec((1, D_MODEL), lambda i, j: (0, 0))],
        out_specs=out_specs,
        out_shape=out_shape,
        scratch_shapes=scratch,
        compiler_params=_params(2),
        name="output_projection_final" if final else "output_projection",
    )(merged, w_out, x, g_next.reshape(1, D_MODEL))


def _pack_kernel(w_ref, tail_ref, fold_ref, o_ref, *, tn):
    j = pl.program_id(1)

    @pl.when(j < FORGET_OFF // tn)
    def _():
        o_ref[...] = (w_ref[...] * fold_ref[...]).astype(o_ref.dtype)

    @pl.when(j >= FORGET_OFF // tn)
    def _():
        wide = jnp.concatenate([w_ref[...], tail_ref[...]], axis=1)
        o_ref[...] = (wide[:, HEADS:HEADS + tn] * fold_ref[...]).astype(o_ref.dtype)


def _pack_input_weights(w_in, tr=1024, tn=1024):
    fold = jnp.ones((PACKED_WIDTH,), F32)
    for seg in (SEG_A_Q, SEG_F_Q):
        fold = fold.at[seg * BRANCH_WIDTH:(seg + 1) * BRANCH_WIDTH].set(Q_FOLD)
    rows = DEPTH * D_MODEL
    w2d = w_in.reshape(rows, w_in.shape[2])
    packed = pl.pallas_call(
        functools.partial(_pack_kernel, tn=tn),
        grid=(rows // tr, PACKED_WIDTH // tn),
        in_specs=[pl.BlockSpec((tr, tn), lambda i, j: (i, j)),
                  pl.BlockSpec((tr, LANES), lambda i, j: (i, (j + 1) * (tn // LANES))),
                  pl.BlockSpec((1, tn), lambda i, j: (0, j))],
        out_specs=pl.BlockSpec((tr, tn), lambda i, j: (i, j)),
        out_shape=jax.ShapeDtypeStruct((rows, PACKED_WIDTH), BF16),
        compiler_params=_params(2),
        name="pack_input_weights",
    )(w2d, w2d, fold.reshape(1, PACKED_WIDTH))
    return packed.reshape(DEPTH, D_MODEL, PACKED_WIDTH)


def _layer(layer, h, hn, w_packed, w_f, b_f, w_pool, pool_scale, w_conv, w_branch, b_merge, w_out,
           g_next, final):
    proj = branch_projection(hn, w_packed, layer)
    c = forget_cumsum(hn, w_f, b_f)
    branches = moba_attention(proj)
    branches = fox_attention(proj, c, branches)
    branches = pool_conv_mixers(proj, w_pool, pool_scale, w_conv, branches)
    merged = gated_merge(hn, w_packed, layer, b_merge, branches, w_branch)
    return output_projection(merged, w_out, h, g_next, final)


def kernel(x, norm_g, w_in, b_forget, w_pool, pool_scale, w_conv, w_branch, b_merge, w_out, final_g):
    w_packed = _pack_input_weights(w_in)
    w_f = jnp.pad(w_in[:, :, FORGET_OFF:FORGET_OFF + HEADS],
                  ((0, 0), (0, 0), (0, LANES - HEADS))).astype(BF16)
    b_f = jnp.pad(b_forget, ((0, 0), (0, LANES - HEADS))).reshape(DEPTH, 1, LANES)
    w_pool_b = w_pool.astype(BF16)
    w_branch_b = w_branch.astype(BF16)
    w_out_b = w_out.astype(BF16)
    g_next = (norm_g[1], final_g)

    h = x.reshape(TOKENS, D_MODEL)
    hn = rmsnorm(h, norm_g[0])
    for layer in range(DEPTH):
        final = layer == DEPTH - 1
        out = _layer(layer, h, hn, w_packed, w_f[layer], b_f[layer], w_pool_b[layer],
                     pool_scale[layer], w_conv[layer], w_branch_b[layer], b_merge[layer],
                     w_out_b[layer], g_next[layer], final)
        if final:
            return out.reshape(BATCH, SEQ, D_MODEL)
        h, hn = out
```
